```python
import numpy as np
import jax
import jax.numpy as jnp
from jax import lax

D_MODEL = 1024
BATCH = 8
SEQ = 2048
DEPTH = 4
DEC_BATCH = 128
DEC_SEQ = 4
PAST_LEN = 16384
PAGE_SIZE = 128

HEAD_DIM = 64
GROUP_W = D_MODEL // 4
N_GH = GROUP_W // HEAD_DIM
D_MIX = 4 * GROUP_W
N_MEM = 256
XA_HEADS = 4
XA_HEAD_DIM = D_MODEL // XA_HEADS
D_FF = 4 * D_MODEL
CONV_W = 4
RGLRU_C = 8.0
RW_DECAY_RANK = D_MODEL // 16
RW_A_RANK = D_MODEL // 16
RW_GATE_RANK = D_MODEL // 8
RWKV_W = 3 * GROUP_W + RW_DECAY_RANK + RW_A_RANK + RW_GATE_RANK
MLSTM_W = 4 * GROUP_W + 2 * N_GH
RET_W = 4 * GROUP_W
RGLRU_W = 2 * GROUP_W
N_IN = MLSTM_W + RET_W + RGLRU_W + RWKV_W
CHUNK = 64
ROPE_BASE = 10000.0
ALPHA = (2.0 * DEPTH) ** 0.25
BETA = (8.0 * DEPTH) ** -0.25
LN_EPS = 1e-5
HEAD_EPS = 1e-6
RWKV_GN_EPS = 64e-5

kernel_name = 'hymba_mlstm_retnet_rglru_rwkv7_decoder_step'

F32 = jnp.float32


def _layer_norm(x, g, b):
    xf = x.astype(F32)
    mu = jnp.mean(xf, -1, keepdims=True)
    var = jnp.mean(jnp.square(xf - mu), -1, keepdims=True)
    return ((xf - mu) * lax.rsqrt(var + LN_EPS) * g + b).astype(x.dtype)


def _head_norm(x, eps):
    xf = x.astype(F32)
    mu = jnp.mean(xf, -1, keepdims=True)
    var = jnp.mean(jnp.square(xf - mu), -1, keepdims=True)
    y = (xf - mu) * lax.rsqrt(var + eps)
    return y.reshape(x.shape[0], x.shape[1], -1)


def _chunk_len(T):
    return CHUNK if T % CHUNK == 0 else T


def _to_chunks(a, L):
    B, T = a.shape[:2]
    return jnp.moveaxis(a.reshape((B, T // L, L) + a.shape[2:]), 1, 0)


def _from_chunks(a):
    n, B, L = a.shape[:3]
    return jnp.moveaxis(a, 0, 1).reshape((B, n * L) + a.shape[3:])


def _rotary(x, pos):
    half = x.shape[-1] // 2
    inv = ROPE_BASE ** (-jnp.arange(half, dtype=F32) / half)
    ang = pos.astype(F32)[:, None] * inv[None, :]
    cos = jnp.cos(ang)[None, :, None, :]
    sin = jnp.sin(ang)[None, :, None, :]
    x1 = x[..., :half].astype(F32)
    x2 = x[..., half:].astype(F32)
    return jnp.concatenate([x1 * cos - x2 * sin, x2 * cos + x1 * sin], axis=-1)


def _mlstm(q, k, v, ig, fg, C0, n0, m0):
    B, T, H, dh = q.shape
    L = _chunk_len(T)
    q = q.astype(F32)
    k = k.astype(F32) * dh ** -0.5
    v = v.astype(F32)
    ig = ig.astype(F32)
    logf = jax.nn.log_sigmoid(fg.astype(F32))
    causal = jnp.tril(jnp.ones((L, L), dtype=bool))[None, :, :, None]

    def step(carry, xs):
        C, n, m = carry
        qc, kc, vc, ic, lf = xs
        b = jnp.cumsum(lf, axis=1)
        d_log = jnp.where(causal, b[:, :, None, :] - b[:, None, :, :] + ic[:, None, :, :], -jnp.inf)
        inter = b + m[:, None, :]
        m_t = jnp.maximum(inter, jnp.max(d_log, axis=2))
        s = jnp.einsum('bthd,bjhd->btjh', qc, kc) * jnp.exp(d_log - m_t[:, :, None, :])
        w_inter = jnp.exp(inter - m_t)
        num = jnp.einsum('btjh,bjhd->bthd', s, vc) + w_inter[..., None] * jnp.einsum('bthk,bhkv->bthv', qc, C)
        den = jnp.sum(s, axis=2) + w_inter * jnp.einsum('bthk,bhk->bth', qc, n)
        h = num / jnp.maximum(jnp.abs(den), jnp.exp(-m_t))[..., None]
        b_end = b[:, -1]
        w_log = b_end[:, None, :] - b + ic
        m_new = jnp.maximum(b_end + m, jnp.max(w_log, axis=1))
        w_j = jnp.exp(w_log - m_new[:, None, :])
        carry_w = jnp.exp(b_end + m - m_new)
        C_new = carry_w[..., None, None] * C + jnp.einsum('bjh,bjhk,bjhv->bhkv', w_j, kc, vc)
        n_new = carry_w[..., None] * n + jnp.einsum('bjh,bjhk->bhk', w_j, kc)
        return (C_new, n_new, m_new), h

    xs = tuple(_to_chunks(a, L) for a in (q, k, v, ig, logf))
    (C1, n1, m1), h = lax.scan(step, (C0.astype(F32), n0.astype(F32), m0.astype(F32)), xs)
    return _from_chunks(h), C1, n1, m1


def _retention(q, k, v, R0):
    B, T, H, dk = q.shape
    L = _chunk_len(T)
    k = k * dk ** -0.5
    v = v.astype(F32)
    log_g = jnp.log(1.0 - 2.0 ** (-5.0 - jnp.arange(H, dtype=F32)))
    idx = jnp.arange(L, dtype=F32)
    diff = idx[:, None] - idx[None, :]
    d_mask = jnp.where(diff[..., None] >= 0, jnp.exp(jnp.maximum(diff, 0.0)[..., None] * log_g), 0.0)
    xi = jnp.exp((idx + 1.0)[:, None] * log_g)
    zeta = jnp.exp((L - 1.0 - idx)[:, None] * log_g)
    g_chunk = jnp.exp(L * log_g)

    def step(R, xs):
        qc, kc, vc = xs
        s = jnp.einsum('bthd,bjhd->btjh', qc, kc) * d_mask
        o = jnp.einsum('btjh,bjhv->bthv', s, vc) + xi[None, :, :, None] * jnp.einsum('bthk,bhkv->bthv', qc, R)
        R_new = g_chunk[None, :, None, None] * R + jnp.einsum('jh,bjhk,bjhv->bhkv', zeta, kc, vc)
        return R_new, o

    R1, o = lax.scan(step, R0.astype(F32), tuple(_to_chunks(a, L) for a in (q, k, v)))
    return _from_chunks(o), R1


def _lin_combine(left, right):
    a1, b1 = left
    a2, b2 = right
    return a1 * a2, a2 * b1 + b2


def _rglru(xb, conv0, h0, conv_w, conv_b, w_a, b_a, w_x, b_x, lam):
    B, T, W = xb.shape
    xp = jnp.concatenate([conv0.astype(F32), xb.astype(F32)], axis=1)
    xc = conv_b + sum(xp[:, i:i + T] * conv_w[i] for i in range(CONV_W))
    new_buf = xp[:, T:]
    xh = xc.reshape(B, T, N_GH, HEAD_DIM)
    r = jax.nn.sigmoid(jnp.einsum('btni,nij->btnj', xh, w_a).reshape(B, T, W) + b_a)
    i = jax.nn.sigmoid(jnp.einsum('btni,nij->btnj', xh, w_x).reshape(B, T, W) + b_x)
    log_a = -RGLRU_C * r * jax.nn.softplus(-lam)
    a = jnp.exp(log_a)
    b = jnp.sqrt(-jnp.expm1(2.0 * log_a)) * (i * xc)
    b = b.at[:, 0].add(a[:, 0] * h0.astype(F32))
    _, h = lax.associative_scan(_lin_combine, (a, b), axis=1)
    return h, h[:, -1], new_buf


def _rwkv7(z, shift0, S0, mu, w0, w2, a0, a2, g2, k_k, k_a, r_k, lnx_g, lnx_b):
    B, T, _ = z.shape
    GW = GROUP_W
    z = z.astype(F32)
    zprev = jnp.concatenate([shift0.astype(F32)[:, None], z[:, :-1]], axis=1)
    zs = z + (zprev - z) * mu
    r, k, v, zw, za, zg = jnp.split(zs, [GW, 2 * GW, 3 * GW, 3 * GW + RW_DECAY_RANK,
                                         3 * GW + RW_DECAY_RANK + RW_A_RANK], axis=-1)
    logw = -jax.nn.softplus(-(w0 + jnp.tanh(zw) @ w2)) - 0.5
    decay = jnp.exp(-jnp.exp(logw))
    a = jax.nn.sigmoid(a0 + za @ a2)
    g = jax.nn.sigmoid(zg) @ g2
    hd = lambda t: t.reshape(B, T, N_GH, HEAD_DIM)
    kk = hd(k * k_k)
    kk = kk / jnp.maximum(jnp.sqrt(jnp.sum(kk * kk, -1, keepdims=True)), 1e-12)
    k = k * (1.0 + (a - 1.0) * k_a)
    r_h, k_h, v_h, w_h, a_h = hd(r), hd(k), hd(v), hd(decay), hd(a)
    tm = lambda t: jnp.moveaxis(t, 1, 0)

    def step(S, inp):
        r_t, k_t, v_t, w_t, kk_t, a_t = inp
        s_kk = jnp.einsum('bhvk,bhk->bhv', S, kk_t)
        S = (S * w_t[:, :, None, :] - s_kk[..., None] * (kk_t * a_t)[:, :, None, :]
             + v_t[..., None] * k_t[:, :, None, :])
        return S, jnp.einsum('bhvk,bhk->bhv', S, r_t)

    S1, o = lax.scan(step, S0.astype(F32), (tm(r_h), tm(k_h), tm(v_h), tm(w_h), tm(kk), tm(a_h)))
    o = jnp.moveaxis(o, 0, 1)
    y = _head_norm(o, RWKV_GN_EPS) * lnx_g + lnx_b
    bonus = jnp.sum(r_h * k_h * r_k, -1, keepdims=True) * v_h
    y = (y + bonus.reshape(B, T, GW)) * g
    return y, S1, z[:, -1]


def _cross_attn(x, mem_k, mem_v, wq, wo):
    B, T, _ = x.shape
    q = (x @ wq).reshape(B, T, XA_HEADS, XA_HEAD_DIM)
    s = jnp.einsum('bthd,bnhd->bhtn', q, mem_k).astype(F32) * XA_HEAD_DIM ** -0.5
    p = jax.nn.softmax(s, axis=-1).astype(x.dtype)
    o = jnp.einsum('bhtn,bnhd->bthd', p, mem_v).reshape(B, T, -1)
    return o @ wo


def _layer(x, pos, mem_k, mem_v, state, P):
    B, T, _ = x.shape
    C0, n0, m0, R0, h0, conv0, S0, shift0 = state
    GW = GROUP_W
    z = x @ P['w_in']
    widths = (GW, GW, GW, GW, N_GH, N_GH, GW, GW, GW, GW, GW, GW, RWKV_W)
    splits = np.cumsum(widths)[:-1].tolist()
    mq, mk, mv, mo, mi, mf, rq, rk, rv, rg, gx, gg, wz = jnp.split(z, splits, axis=-1)
    hd = lambda t: t.reshape(B, T, N_GH, HEAD_DIM)
    hm, C1, n1, m1 = _mlstm(hd(mq), hd(mk), hd(mv), mi + P['mlstm_b_i'], mf + P['mlstm_b_f'], C0, n0, m0)
    y_m = jax.nn.sigmoid(mo.astype(F32)) * (_head_norm(hm, HEAD_EPS) * P['mlstm_g'])
    ro, R1 = _retention(_rotary(hd(rq), pos), _rotary(hd(rk), pos), hd(rv), R0)
    y_r = jax.nn.silu(rg.astype(F32)) * (_head_norm(ro, HEAD_EPS) * P['ret_g'])
    hg, h1, conv1 = _rglru(gx, conv0, h0, P['conv_w'], P['conv_b'], P['rg_w_a'], P['rg_b_a'],
                           P['rg_w_x'], P['rg_b_x'], P['rg_lam'])
    y_g = hg * jax.nn.gelu(gg.astype(F32))
    y_w, S1, shift1 = _rwkv7(wz, shift0, S0, P['rwkv_mu'], P['rwkv_w0'], P['rwkv_w2'], P['rwkv_a0'],
                             P['rwkv_a2'], P['rwkv_g2'], P['rwkv_k_k'], P['rwkv_k_a'], P['rwkv_r_k'],
                             P['rwkv_lnx_g'], P['rwkv_lnx_b'])
    mix = jnp.concatenate([y_m, y_r, y_g, y_w], axis=-1).astype(x.dtype) @ P['w_out']
    x = _layer_norm(ALPHA * x + mix, P['ln1_g'], P['ln1_b'])
    x = _layer_norm(ALPHA * x + _cross_attn(x, mem_k, mem_v, P['xa_wq'], P['xa_wo']), P['ln2_g'], P['ln2_b'])
    ff = jnp.square(jax.nn.relu(x @ P['w_up'])) @ P['w_down']
    x = _layer_norm(ALPHA * x + ff, P['ln3_g'], P['ln3_b'])
    dt = x.dtype
    new = (C1.astype(dt), n1.astype(dt), m1.astype(dt), R1.astype(dt), h1.astype(dt),
           conv1.astype(dt), S1.astype(dt), shift1.astype(dt))
    return x, new


def setup_inputs(seed: int = 0) -> dict:
    key = jax.random.key(seed)
    keys = iter(jax.random.split(key, 64))

    def nrm(shape, scale):
        return jax.random.normal(next(keys), shape, F32) * scale

    def gain(shape):
        return 1.0 + nrm(shape, 0.01)

    L, H, dh, GW, D = DEPTH, N_GH, HEAD_DIM, GROUP_W, D_MODEL
    xav = lambda fi, fo: BETA * (2.0 / (fi + fo)) ** 0.5
    u_lru = jax.random.uniform(next(keys), (L, GW), F32, 0.9, 0.999)
    a_base = u_lru ** (1.0 / RGLRU_C)
    rg_lam = jnp.log(a_base) - jnp.log1p(-a_base)
    return {
        'x_prompt': nrm((BATCH, SEQ, D), 1.0),
        'x_sample': nrm((DEC_BATCH, DEC_SEQ, D), 1.0),
        'state_mlstm_C': nrm((L, DEC_BATCH, H, dh, dh), 0.05),
        'state_mlstm_n': nrm((L, DEC_BATCH, H, dh), 0.05),
        'state_mlstm_m': nrm((L, DEC_BATCH, H), 1.0),
        'state_ret': nrm((L, DEC_BATCH, H, dh, dh), 0.1),
        'state_rglru_h': nrm((L, DEC_BATCH, GW), 0.5),
        'state_rglru_conv': nrm((L, DEC_BATCH, CONV_W - 1, GW), 1.0),
        'state_rwkv_S': nrm((L, DEC_BATCH, H, dh, dh), 0.1),
        'state_rwkv_shift': nrm((L, DEC_BATCH, RWKV_W), 1.0),
        'cache_mem_k': nrm((L, DEC_BATCH, N_MEM, XA_HEADS, XA_HEAD_DIM), 1.0),
        'cache_mem_v': nrm((L, DEC_BATCH, N_MEM, XA_HEADS, XA_HEAD_DIM), 0.5),
        'mem_prompt': nrm((BATCH, N_MEM, D), 1.0),
        'ln_in_g': gain((D,)),
        'ln_in_b': nrm((D,), 0.01),
        'w_in': nrm((L, D, N_IN), D ** -0.5),
        'mlstm_b_i': nrm((L, H), 0.1) - 1.0,
        'mlstm_b_f': jnp.linspace(3.0, 6.0, H, dtype=F32) + nrm((L, H), 0.1),
        'mlstm_g': gain((L, GW)),
        'ret_g': gain((L, GW)),
        'conv_w': nrm((L, CONV_W, GW), CONV_W ** -0.5),
        'conv_b': nrm((L, GW), 0.01),
        'rg_w_a': nrm((L, H, dh, dh), dh ** -0.5),
        'rg_b_a': nrm((L, GW), 0.01),
        'rg_w_x': nrm((L, H, dh, dh), dh ** -0.5),
        'rg_b_x': nrm((L, GW), 0.01),
        'rg_lam': rg_lam,
        'rwkv_mu': jax.random.uniform(next(keys), (L, RWKV_W), F32),
        'rwkv_w0': jnp.linspace(-5.0, 1.0, GW, dtype=F32) + nrm((L, GW), 0.1),
        'rwkv_w2': nrm((L, RW_DECAY_RANK, GW), 0.1),
        'rwkv_a0': nrm((L, GW), 0.1),
        'rwkv_a2': nrm((L, RW_A_RANK, GW), 0.1),
        'rwkv_g2': nrm((L, RW_GATE_RANK, GW), RW_GATE_RANK ** -0.5),
        'rwkv_k_k': 0.85 + nrm((L, GW), 0.02),
        'rwkv_k_a': 1.0 + nrm((L, GW), 0.02),
        'rwkv_r_k': nrm((L, H, dh), 0.1),
        'rwkv_lnx_g': gain((L, GW)),
        'rwkv_lnx_b': nrm((L, GW), 0.01),
        'w_out': nrm((L, D_MIX, D), xav(D_MIX, D)),
        'ln1_g': gain((L, D)),
        'ln1_b': nrm((L, D), 0.01),
        'xa_wq': nrm((L, D, D), D ** -0.5),
        'xa_wk': nrm((L, D, D), D ** -0.5),
        'xa_wv': nrm((L, D, D), xav(D, D)),
        'xa_wo': nrm((L, D, D), xav(D, D)),
        'ln2_g': gain((L, D)),
        'ln2_b': nrm((L, D), 0.01),
        'w_up': nrm((L, D, D_FF), xav(D, D_FF)),
        'w_down': nrm((L, D_FF, D), xav(D_FF, D)),
        'ln3_g': gain((L, D)),
        'ln3_b': nrm((L, D), 0.01),
    }


def reference(x_prompt, x_sample, state_mlstm_C, state_mlstm_n, state_mlstm_m, state_ret,
              state_rglru_h, state_rglru_conv, state_rwkv_S, state_rwkv_shift, cache_mem_k,
              cache_mem_v, mem_prompt, ln_in_g, ln_in_b, w_in, mlstm_b_i, mlstm_b_f, mlstm_g,
              ret_g, conv_w, conv_b, rg_w_a, rg_b_a, rg_w_x, rg_b_x, rg_lam, rwkv_mu, rwkv_w0,
              rwkv_w2, rwkv_a0, rwkv_a2, rwkv_g2, rwkv_k_k, rwkv_k_a, rwkv_r_k, rwkv_lnx_g,
              rwkv_lnx_b, w_out, ln1_g, ln1_b, xa_wq, xa_wk, xa_wv, xa_wo, ln2_g, ln2_b, w_up,
              w_down, ln3_g, ln3_b):
    Bp, Tp, _ = x_prompt.shape
    Ts = x_sample.shape[1]
    pos_p = jnp.arange(Tp, dtype=jnp.int32)
    pos_s = PAST_LEN + jnp.arange(Ts, dtype=jnp.int32)
    dt = x_prompt.dtype
    zero_state = (jnp.zeros((Bp, N_GH, HEAD_DIM, HEAD_DIM), dt), jnp.zeros((Bp, N_GH, HEAD_DIM), dt),
                  jnp.zeros((Bp, N_GH), dt), jnp.zeros((Bp, N_GH, HEAD_DIM, HEAD_DIM), dt),
                  jnp.zeros((Bp, GROUP_W), dt), jnp.zeros((Bp, CONV_W - 1, GROUP_W), dt),
                  jnp.zeros((Bp, N_GH, HEAD_DIM, HEAD_DIM), dt), jnp.zeros((Bp, RWKV_W), dt))
    xp = _layer_norm(x_prompt, ln_in_g, ln_in_b)
    xs = _layer_norm(x_sample, ln_in_g, ln_in_b)
    new_p, new_s, mk_p, mv_p = [], [], [], []
    for l in range(DEPTH):
        P = {'w_in': w_in[l], 'mlstm_b_i': mlstm_b_i[l], 'mlstm_b_f': mlstm_b_f[l], 'mlstm_g': mlstm_g[l],
             'ret_g': ret_g[l], 'conv_w': conv_w[l], 'conv_b': conv_b[l], 'rg_w_a': rg_w_a[l],
             'rg_b_a': rg_b_a[l], 'rg_w_x': rg_w_x[l], 'rg_b_x': rg_b_x[l], 'rg_lam': rg_lam[l],
             'rwkv_mu': rwkv_mu[l], 'rwkv_w0': rwkv_w0[l], 'rwkv_w2': rwkv_w2[l], 'rwkv_a0': rwkv_a0[l],
             'rwkv_a2': rwkv_a2[l], 'rwkv_g2': rwkv_g2[l], 'rwkv_k_k': rwkv_k_k[l], 'rwkv_k_a': rwkv_k_a[l],
             'rwkv_r_k': rwkv_r_k[l], 'rwkv_lnx_g': rwkv_lnx_g[l], 'rwkv_lnx_b': rwkv_lnx_b[l],
             'w_out': w_out[l], 'ln1_g': ln1_g[l], 'ln1_b': ln1_b[l], 'xa_wq': xa_wq[l], 'xa_wo': xa_wo[l],
             'ln2_g': ln2_g[l], 'ln2_b': ln2_b[l], 'w_up': w_up[l], 'w_down': w_down[l],
             'ln3_g': ln3_g[l], 'ln3_b': ln3_b[l]}
        kp = (mem_prompt @ xa_wk[l]).reshape(Bp, -1, XA_HEADS, XA_HEAD_DIM)
        vp = (mem_prompt @ xa_wv[l]).reshape(Bp, -1, XA_HEADS, XA_HEAD_DIM)
        xp, st_p = _layer(xp, pos_p, kp, vp, zero_state, P)
        st_in = (state_mlstm_C[l], state_mlstm_n[l], state_mlstm_m[l], state_ret[l], state_rglru_h[l],
                 state_rglru_conv[l], state_rwkv_S[l], state_rwkv_shift[l])
        xs, st_s = _layer(xs, pos_s, cache_mem_k[l], cache_mem_v[l], st_in, P)
        new_p.append(st_p)
        new_s.append(st_s)
        mk_p.append(kp)
        mv_p.append(vp)
    p_C, p_n, p_m, p_R, p_h, p_conv, p_S, p_shift = [jnp.stack(s) for s in zip(*new_p)]
    s_C, s_n, s_m, s_R, s_h, s_conv, s_S, s_shift = [jnp.stack(s) for s in zip(*new_s)]
    p_mem_k = jnp.stack(mk_p)
    p_mem_v = jnp.stack(mv_p)
    return (xp, xs, p_C, p_n, p_m, p_R, p_h, p_conv, p_S, p_shift, p_mem_k, p_mem_v,
            s_C, s_n, s_m, s_R, s_h, s_conv, s_S, s_shift)
```

```python
import functools
from typing import NamedTuple

import numpy as np
import jax
import jax.numpy as jnp
from jax import lax
from jax.experimental import pallas as pl
from jax.experimental.pallas import tpu as pltpu

F32 = jnp.float32
BF16 = jnp.bfloat16
HI = lax.Precision.HIGHEST

D_MODEL = 1024
DEPTH = 4
DEC_SEQ = 4
PAST_LEN = 16384
HEAD_DIM = 64
GROUP_W = 256
N_GH = 4
N_MEM = 256
XA_HEADS = 4
XA_HEAD_DIM = 256
D_FF = 4096
CONV_W = 4
RGLRU_C = 8.0
ROPE_BASE = 10000.0
ALPHA = (2.0 * DEPTH) ** 0.25
LN_EPS = 1e-5
HEAD_EPS = 1e-6
RWKV_GN_EPS = 64e-5
NEG = -1e30

ROWS = 128
SAMPLE_L = 8
SAMPLE_LO = SAMPLE_L - DEC_SEQ
VMEM_LIMIT = 48 * 1024 * 1024

COL_M, COL_R, COL_W, COL_G, COL_I, COL_F, Z_W = 0, 1024, 2048, 3072, 3584, 3712, 3840
O_MQ, O_MI, O_MF, O_RQ, O_RK, O_RV, O_RG, O_GX, O_WZ, N_IN = 0, 1024, 1028, 1032, 1288, 1544, 1800, 2056, 2568, 3592


class _Cfg(NamedTuple):
    S: int
    C: int
    G: int
    L: int
    lo: int


def _dot(a, b):
    return jnp.dot(a.astype(BF16), b.astype(BF16), preferred_element_type=F32)


def _dot_nt(a, b):
    return lax.dot_general(a.astype(BF16), b.astype(BF16), (((1,), (1,)), ((), ())),
                           preferred_element_type=F32)


def _dot_tn(a, b):
    return lax.dot_general(a.astype(BF16), b.astype(BF16), (((0,), (0,)), ((), ())),
                           preferred_element_type=F32)


def _dot_hi(a, b):
    return jnp.dot(a, b, precision=HI, preferred_element_type=F32)


def _dot_nt_hi(a, b):
    return lax.dot_general(a, b, (((1,), (1,)), ((), ())), precision=HI,
                           preferred_element_type=F32)


def _softplus(x):
    return jnp.maximum(x, 0.0) + jnp.log1p(jnp.exp(-jnp.abs(x)))


def _expm1(x):
    u = jnp.exp(x)
    edge = (u == 1.0) | (u == 0.0)
    body = (u - 1.0) * x / jnp.log(jnp.where(edge, 2.0, u))
    return jnp.where(u == 1.0, x, jnp.where(u == 0.0, -1.0, body))


def _sigmoid(x):
    return 1.0 / (1.0 + jnp.exp(-x))


def _ln_rows(x, g, b):
    mu = jnp.mean(x, -1, keepdims=True)
    xc = x - mu
    var = jnp.mean(xc * xc, -1, keepdims=True)
    return xc * lax.rsqrt(var + LN_EPS) * g + b


def _params(sem):
    return pltpu.CompilerParams(dimension_semantics=sem, vmem_limit_bytes=VMEM_LIMIT)


def _full(shape):
    nd = len(shape)
    return pl.BlockSpec(shape, lambda *_: (0,) * nd)


def _ln_kernel(x_ref, g_ref, b_ref, o_ref):
    o_ref[...] = _ln_rows(x_ref[...], g_ref[...], b_ref[...])


def _layer_norm(x, g, b, tm):
    M, K = x.shape
    return pl.pallas_call(
        _ln_kernel, grid=(M // tm,),
        in_specs=[pl.BlockSpec((tm, K), lambda i: (i, 0)), _full((1, K)), _full((1, K))],
        out_specs=pl.BlockSpec((tm, K), lambda i: (i, 0)),
        out_shape=jax.ShapeDtypeStruct((M, K), F32),
        compiler_params=_params(("parallel",)), name="layer_norm")(x, g, b)


def _mm_kernel(x_ref, w_ref, o_ref, *, chunks):
    xb = x_ref[...].astype(BF16)
    for c0, cw in chunks:
        o_ref[:, c0:c0 + cw] = jnp.dot(xb, w_ref[:, c0:c0 + cw],
                                       preferred_element_type=F32).astype(o_ref.dtype)


def _matmul(x, w, out_dtype, tm, name):
    M, K = x.shape
    N = w.shape[1]
    chunks = tuple((c, min(512, N - c)) for c in range(0, N, 512))
    return pl.pallas_call(
        functools.partial(_mm_kernel, chunks=chunks), grid=(M // tm,),
        in_specs=[pl.BlockSpec((tm, K), lambda i: (i, 0)), _full((K, N))],
        out_specs=pl.BlockSpec((tm, N), lambda i: (i, 0)),
        out_shape=jax.ShapeDtypeStruct((M, N), out_dtype),
        compiler_params=_params(("parallel",)), name=name)(x, w)


def _proj_ln_kernel(*refs, nparts):
    a_refs, w_refs = refs[:nparts], refs[nparts:2 * nparts]
    x_ref, g_ref, b_ref, o_ref = refs[2 * nparts:]
    acc = ALPHA * x_ref[...]
    for a, w in zip(a_refs, w_refs):
        acc = acc + jnp.dot(a[...].astype(BF16), w[...], preferred_element_type=F32)
    o_ref[...] = _ln_rows(acc, g_ref[...], b_ref[...])


def _proj_res_ln(parts, ws, x, g, b, tm, name):
    M, K = x.shape
    n = len(parts)
    in_specs = ([pl.BlockSpec((tm, p.shape[1]), lambda i: (i, 0)) for p in parts]
                + [_full(w.shape) for w in ws]
                + [pl.BlockSpec((tm, K), lambda i: (i, 0)), _full((1, K)), _full((1, K))])
    return pl.pallas_call(
        functools.partial(_proj_ln_kernel, nparts=n), grid=(M // tm,),
        in_specs=in_specs, out_specs=pl.BlockSpec((tm, K), lambda i: (i, 0)),
        out_shape=jax.ShapeDtypeStruct((M, K), F32),
        compiler_params=_params(("parallel",)), name=name)(*parts, *ws, x, g, b)


def _mlp_kernel(x_ref, wu_ref, wd_ref, g_ref, b_ref, o_ref, *, ff_chunk):
    x = x_ref[...]
    xb = x.astype(BF16)
    acc = ALPHA * x
    for c0 in range(0, D_FF, ff_chunk):
        h = jnp.dot(xb, wu_ref[:, c0:c0 + ff_chunk], preferred_element_type=F32)
        h = jnp.square(jnp.maximum(h, 0.0)).astype(BF16)
        acc = acc + jnp.dot(h, wd_ref[c0:c0 + ff_chunk, :], preferred_element_type=F32)
    o_ref[...] = _ln_rows(acc, g_ref[...], b_ref[...])


def _mlp(x, wu, wd, g, b, tm):
    M, K = x.shape
    return pl.pallas_call(
        functools.partial(_mlp_kernel, ff_chunk=1024), grid=(M // tm,),
        in_specs=[pl.BlockSpec((tm, K), lambda i: (i, 0)),
                  pl.BlockSpec(wu.shape, lambda i: (0, 0), pipeline_mode=pl.Buffered(1)),
                  pl.BlockSpec(wd.shape, lambda i: (0, 0), pipeline_mode=pl.Buffered(1)),
                  _full((1, K)), _full((1, K))],
        out_specs=pl.BlockSpec((tm, K), lambda i: (i, 0)),
        out_shape=jax.ShapeDtypeStruct((M, K), F32),
        compiler_params=_params(("parallel",)), name="mlp")(x, wu, wd, g, b)


def _attn_kernel(q_ref, k_ref, v_ref, o_ref, *, G):
    scale = XA_HEAD_DIM ** -0.5
    for g in range(G):
        for h in range(XA_HEADS):
            sl = slice(h * XA_HEAD_DIM, (h + 1) * XA_HEAD_DIM)
            q = q_ref[g, :, sl]
            s = _dot_nt(q, k_ref[g, :, sl]) * scale
            e = jnp.exp(s - jnp.max(s, -1, keepdims=True))
            p = e / jnp.sum(e, -1, keepdims=True)
            o_ref[g, :, sl] = _dot(p, v_ref[g, :, sl]).astype(o_ref.dtype)


def _cross_attn(q, mem_k, mem_v, G, tq):
    B, T, K = q.shape
    return pl.pallas_call(
        functools.partial(_attn_kernel, G=G), grid=(B // G, T // tq),
        in_specs=[pl.BlockSpec((G, tq, K), lambda b, t: (b, t, 0)),
                  pl.BlockSpec((G, N_MEM, K), lambda b, t: (b, 0, 0)),
                  pl.BlockSpec((G, N_MEM, K), lambda b, t: (b, 0, 0))],
        out_specs=pl.BlockSpec((G, tq, K), lambda b, t: (b, t, 0)),
        out_shape=jax.ShapeDtypeStruct((B, T, K), BF16),
        compiler_params=_params(("parallel", "parallel")), name="cross_attn")(q, mem_k, mem_v)


def _head_masks():
    lane = np.arange(GROUP_W)
    mv = (lane[None, :] // HEAD_DIM == np.arange(N_GH)[:, None]).astype(np.float32)
    mkr = ((lane[None, :] % 128) // 32 == np.arange(N_GH)[:, None]).astype(np.float32)
    return mv, mkr


def _ret_key_std():
    r = np.arange(GROUP_W)
    return ((r % 128) // 32) * HEAD_DIM + (r // 128) * 32 + (r % 32)


@functools.lru_cache(maxsize=None)
def _consts(cfg):
    G, L, lo = cfg.G, cfg.L, cfg.lo
    R = G * L
    assert R == ROWS
    r = np.arange(R)
    seg, p = r // L, r % L
    valid = p >= lo
    same = seg[:, None] == seg[None, :]
    le = r[None, :] <= r[:, None]
    tri = (same & le).astype(np.float32)
    minc = (same & le & valid[None, :]).astype(np.float32)
    mstr = (same & (r[None, :] < r[:, None]) & valid[None, :]).astype(np.float32)
    last = seg * L + L - 1
    seglast = (r[None, :] == last[:, None]).astype(np.float32)
    gi = np.arange(ROWS)
    sellast = (r[None, :] == (gi[:, None] * L + L - 1)).astype(np.float32) * (gi[:, None] < G)
    segexp = (seg[:, None] == gi[None, :]).astype(np.float32)
    segsum = segexp.T.copy()
    rowsel = (r[None, :] == (np.repeat(np.arange(G), GROUP_W)[:, None] * L + L - 1)).astype(np.float32)
    mv, mkr = _head_masks()
    blockones = (np.arange(GROUP_W)[:, None] // HEAD_DIM == np.arange(GROUP_W)[None, :] // HEAD_DIM)
    blockones = blockones.astype(np.float32)
    khead = (np.arange(GROUP_W) % 128) // 32
    bdr = (khead[:, None] == np.arange(GROUP_W)[None, :] // HEAD_DIM).astype(np.float32)
    expand = np.zeros((128, GROUP_W), np.float32)
    for h in range(N_GH):
        expand[h, h * HEAD_DIM:(h + 1) * HEAD_DIM] = 1.0
    oh = np.zeros((N_GH, 128), np.float32)
    ohb = np.zeros((N_GH, 128), np.float32)
    for h in range(N_GH):
        oh[h, h] = 1.0
        ohb[h, N_GH + h] = 1.0
    lane_a = (np.arange(128) < N_GH).astype(np.float32)[None, :]
    lane_b = ((np.arange(128) >= N_GH) & (np.arange(128) < 2 * N_GH)).astype(np.float32)[None, :]
    log_g = np.log(1.0 - 2.0 ** (-5.0 - np.arange(N_GH, dtype=np.float64)))
    dpos = (p[:, None] - p[None, :]).astype(np.float64)
    dmask4 = np.concatenate([np.exp(np.maximum(dpos, 0.0) * log_g[h]) * minc for h in range(N_GH)], 0)
    pv = (p - lo).astype(np.float64)
    hv = np.arange(GROUP_W) // HEAD_DIM
    xi = np.exp((pv[:, None] + 1.0) * log_g[hv][None, :])
    zeta = np.exp((L - lo - 1.0 - pv)[:, None] * log_g[khead][None, :]) * valid[:, None]
    gdec = np.exp((L - lo) * log_g[hv])[None, :]
    c = dict(
        tri=tri, minc4=np.tile(minc, (4, 1)), mstr4=np.tile(mstr, (4, 1)), seglast=seglast,
        sellast=sellast, segexp=segexp, segsum=segsum, rowsel=rowsel,
        valid=np.tile(valid[:, None].astype(np.float32), (1, GROUP_W)),
        posb=np.tile(p[:, None].astype(np.float32), (1, GROUP_W)),
        mv=mv, mkr=mkr, blockones=blockones, avg=blockones / HEAD_DIM,
        bd=np.tile(blockones, (G, 1)), bdr=np.tile(bdr, (G, 1)), expand=expand, oh=oh, ohb=ohb,
        lane_a=lane_a, lane_b=lane_b, ones=np.ones((128, 128), np.float32),
        dmask4=dmask4, xi=xi, zeta=zeta, gdec=gdec)
    if G > 1:
        c["segx"] = np.repeat(segexp[:, :G], GROUP_W, axis=1)
    else:
        c["segx"] = np.ones((R, GROUP_W), np.float32)
    return {k: np.asarray(v, np.float32) for k, v in c.items()}


def _expand(x, segx_ref, G):
    if G == 1:
        return x
    return jnp.concatenate([x] * G, axis=1) * segx_ref[...]


def _select(xg, segx_ref, G):
    if G == 1:
        return xg
    out = xg[:, 0:GROUP_W] * segx_ref[:, 0:GROUP_W]
    for g in range(1, G):
        out = out + xg[:, g * GROUP_W:(g + 1) * GROUP_W] * segx_ref[:, g * GROUP_W:(g + 1) * GROUP_W]
    return out


def _stack_heads(x, mask_ref):
    return jnp.concatenate([x * mask_ref[h:h + 1, :] for h in range(N_GH)], axis=0)


def _unstack_heads(x4, mask_ref):
    out = x4[0:ROWS] * mask_ref[0:1, :]
    for h in range(1, N_GH):
        out = out + x4[h * ROWS:(h + 1) * ROWS] * mask_ref[h:h + 1, :]
    return out


def _rep2(x):
    return jnp.concatenate([x, x], axis=1)


def _head_norm(o, avg_ref, eps):
    mu = _dot_hi(o, avg_ref[...])
    xc = o - mu
    var = _dot_hi(xc * xc, avg_ref[...])
    return xc * lax.rsqrt(var + eps)


def _shift_rows(x, tail, i):
    ext = jnp.concatenate([tail, x], axis=0)
    return pltpu.roll(ext, i, 0)[8:]


def _seg_scan_steps(L):
    d, out = 1, []
    while d < L:
        out.append(d)
        d *= 2
    return out


def _ret_kernel(z_ref, cos_ref, sin_ref, r0_ref, g_ref, dm_ref, xi_ref, zeta_ref, gdec_ref,
                mk_ref, mv_ref, bd_ref, avg_ref, segx_ref, y_ref, r1_ref, st_ref, *, G, C):
    c = pl.program_id(1)

    @pl.when(c == 0)
    def _():
        st_ref[...] = r0_ref[0]

    cs, sn = cos_ref[...], sin_ref[...]
    qa, qb = z_ref[:, 0:128], z_ref[:, 128:256]
    ka, kb = z_ref[:, 256:384], z_ref[:, 384:512]
    v = z_ref[:, 512:768]
    q = jnp.concatenate([qa * cs - qb * sn, qb * cs + qa * sn], axis=1)
    k = jnp.concatenate([ka * cs - kb * sn, kb * cs + ka * sn], axis=1) * HEAD_DIM ** -0.5
    s = _dot_nt(_stack_heads(q, mk_ref), k) * dm_ref[...]
    o = _unstack_heads(_dot(s, v), mv_ref)
    st = st_ref[...]
    inter = _dot(_expand(q, segx_ref, G), st)
    o = o + xi_ref[...] * inter
    upd = _dot_tn(_expand(k * zeta_ref[...], segx_ref, G), v)
    st_ref[...] = st * gdec_ref[...] + upd * bd_ref[...]
    gate = z_ref[:, 768:1024]
    y = gate * _sigmoid(gate) * (_head_norm(o, avg_ref, HEAD_EPS) * g_ref[...])
    y_ref[...] = y.astype(y_ref.dtype)

    @pl.when(c == C - 1)
    def _():
        r1_ref[0] = st_ref[...]


def _retention(z, cos, sin, r0, gain, cfg):
    S, C, G = cfg.S, cfg.C, cfg.G
    k = _consts(cfg)
    consts = [k["dmask4"], k["xi"], k["zeta"], k["gdec"], k["mkr"], k["mv"], k["bdr"], k["avg"], k["segx"]]
    st_shape = (G * GROUP_W, GROUP_W)
    y, r1 = pl.pallas_call(
        functools.partial(_ret_kernel, G=G, C=C), grid=(S, C),
        in_specs=[pl.BlockSpec((ROWS, 1024), lambda s, c: (s * C + c, COL_R // 1024)),
                  pl.BlockSpec((ROWS, 128), lambda s, c: (c, 0)),
                  pl.BlockSpec((ROWS, 128), lambda s, c: (c, 0)),
                  pl.BlockSpec((1,) + st_shape, lambda s, c: (s, 0, 0)),
                  _full((1, GROUP_W))] + [_full(a.shape) for a in consts],
        out_specs=[pl.BlockSpec((ROWS, GROUP_W), lambda s, c: (s * C + c, 0)),
                   pl.BlockSpec((1,) + st_shape, lambda s, c: (s, 0, 0))],
        out_shape=[jax.ShapeDtypeStruct((S * C * ROWS, GROUP_W), BF16),
                   jax.ShapeDtypeStruct((S,) + st_shape, F32)],
        scratch_shapes=[pltpu.VMEM(st_shape, F32)],
        compiler_params=_params(("parallel", "arbitrary")), name="retention",
    )(z, cos, sin, r0, gain, *consts)
    return y, r1


def _mlstm_kernel(z_ref, zi_ref, zf_ref, bi_ref, bf_ref, g_ref, c0_ref, n0_ref, m0_ref,
                  tri_ref, minc4_ref, seglast_ref, sellast_ref, segexp_ref, segsum_ref, rowsel_ref,
                  valid_ref, posb_ref, mv_ref, bd_ref, avg_ref, exp_ref, oh_ref, ohb_ref,
                  la_ref, lb_ref, ones_ref, segx_ref,
                  y_ref, c1_ref, n1_ref, m1_ref, c_st, n_st, m_st, *, G, L, C):
    c = pl.program_id(1)

    @pl.when(c == 0)
    def _():
        c_st[...] = c0_ref[0]
        n_st[...] = n0_ref[0]
        m_st[...] = m0_ref[0]

    q = z_ref[:, 0:256]
    k = z_ref[:, 256:512] * HEAD_DIM ** -0.5
    v = z_ref[:, 512:768]
    valid = valid_ref[:, 0:128]
    ic = jnp.where(valid > 0.5, zi_ref[...] + bi_ref[...], NEG)
    lf = valid * (-_softplus(-(zf_ref[...] + bf_ref[...])))
    b = _dot_hi(tri_ref[...], lf)
    lhs = jnp.concatenate([b * oh_ref[h:h + 1, :] + ohb_ref[h:h + 1, :] for h in range(N_GH)], axis=0)
    g2 = jnp.where(la_ref[...] > 0.5, 1.0, jnp.where(lb_ref[...] > 0.5, ic - b, 0.0))
    d = jnp.where(minc4_ref[...] > 0.5, _dot_nt_hi(lhs, g2), NEG)
    m_rows = _dot_hi(segexp_ref[...], m_st[...])
    inter = b + m_rows
    inter4 = _dot_hi(jnp.concatenate([inter * oh_ref[h:h + 1, :] for h in range(N_GH)], axis=0),
                     ones_ref[...])
    m4 = jnp.maximum(inter4, jnp.max(d, -1, keepdims=True))
    q4 = _stack_heads(q, mv_ref)
    sw = _dot_nt(q4, k) * jnp.exp(d - m4)
    wi4 = jnp.exp(inter4 - m4)
    num4 = _dot(sw, v)
    n_rows = _dot_hi(segexp_ref[...], n_st[...])
    qn4 = jnp.sum(q4 * jnp.concatenate([n_rows] * N_GH, axis=0), -1, keepdims=True)
    den4 = jnp.sum(sw, -1, keepdims=True) + wi4 * qn4
    den4 = jnp.maximum(jnp.abs(den4), jnp.exp(-m4))
    qc = _dot(_expand(q, segx_ref, G), c_st[...])
    hout = None
    for h in range(N_GH):
        rs = slice(h * ROWS, (h + 1) * ROWS)
        part = (num4[rs] + _rep2(wi4[rs]) * qc) / _rep2(den4[rs]) * mv_ref[h:h + 1, :]
        hout = part if hout is None else hout + part
    og = z_ref[:, 768:1024]
    y = _sigmoid(og) * (_head_norm(hout, avg_ref, HEAD_EPS) * g_ref[...])
    y_ref[...] = y.astype(y_ref.dtype)
    b_end = _dot_hi(seglast_ref[...], b)
    wlog = b_end - b + ic
    mx = wlog
    for dd in _seg_scan_steps(L):
        mx = jnp.maximum(mx, jnp.where(posb_ref[:, 0:128] >= dd, pltpu.roll(mx, dd, 0), NEG))
    m_new = jnp.maximum(b_end + m_rows, _dot_hi(seglast_ref[...], mx))
    wj = jnp.exp(wlog - m_new)
    cw = jnp.exp(b_end + m_rows - m_new)
    kw = k * _dot_hi(wj, exp_ref[...])
    cw_b = _dot_hi(cw, exp_ref[...])
    upd = _dot_tn(_expand(kw, segx_ref, G), v)
    c_st[...] = c_st[...] * _dot_hi(rowsel_ref[...], cw_b) + upd * bd_ref[...]
    n_st[...] = n_st[...] * _dot_hi(sellast_ref[...], cw_b) + _dot_hi(segsum_ref[...], kw)
    m_st[...] = _dot_hi(sellast_ref[...], m_new)

    @pl.when(c == C - 1)
    def _():
        c1_ref[0] = c_st[...]
        n1_ref[0] = n_st[...]
        m1_ref[0] = m_st[...]


def _mlstm(z, bi, bf, gain, c0, n0, m0, cfg):
    S, C, G, L = cfg.S, cfg.C, cfg.G, cfg.L
    k = _consts(cfg)
    consts = [k[n] for n in ("tri", "minc4", "seglast", "sellast", "segexp", "segsum", "rowsel", "valid",
                             "posb", "mv", "bd", "avg", "expand", "oh", "ohb", "lane_a", "lane_b", "ones",
                             "segx")]
    cs, ns, ms = (G * GROUP_W, GROUP_W), (ROWS, GROUP_W), (ROWS, 128)
    st_spec = lambda shp: pl.BlockSpec((1,) + shp, lambda s, c: (s, 0, 0))
    return pl.pallas_call(
        functools.partial(_mlstm_kernel, G=G, L=L, C=C), grid=(S, C),
        in_specs=[pl.BlockSpec((ROWS, 1024), lambda s, c: (s * C + c, COL_M // 1024)),
                  pl.BlockSpec((ROWS, 128), lambda s, c: (s * C + c, COL_I // 128)),
                  pl.BlockSpec((ROWS, 128), lambda s, c: (s * C + c, COL_F // 128)),
                  _full((1, 128)), _full((1, 128)), _full((1, GROUP_W)),
                  st_spec(cs), st_spec(ns), st_spec(ms)] + [_full(a.shape) for a in consts],
        out_specs=[pl.BlockSpec((ROWS, GROUP_W), lambda s, c: (s * C + c, 0)),
                   st_spec(cs), st_spec(ns), st_spec(ms)],
        out_shape=[jax.ShapeDtypeStruct((S * C * ROWS, GROUP_W), BF16),
                   jax.ShapeDtypeStruct((S,) + cs, F32), jax.ShapeDtypeStruct((S,) + ns, F32),
                   jax.ShapeDtypeStruct((S,) + ms, F32)],
        scratch_shapes=[pltpu.VMEM(cs, F32), pltpu.VMEM(ns, F32), pltpu.VMEM(ms, F32)],
        compiler_params=_params(("parallel", "arbitrary")), name="mlstm",
    )(z, z, z, bi, bf, gain, c0, n0, m0, *consts)


def _rglru_kernel(z_ref, h0_ref, cw_ref, cb_ref, wa_ref, ba_ref, wx_ref, bx_ref, lam_ref,
                  valid_ref, posb_ref, segexp_ref, sellast_ref,
                  y_ref, h1_ref, tail_ref, h_st, *, L, C):
    c = pl.program_id(1)

    @pl.when(c == 0)
    def _():
        tail_ref[...] = jnp.zeros_like(tail_ref)
        h_st[...] = h0_ref[0]

    gx = z_ref[:, 0:256]
    tail = tail_ref[...]
    xc = cb_ref[...] + gx * cw_ref[CONV_W - 1:CONV_W, :]
    for i in range(CONV_W - 1):
        xc = xc + _shift_rows(gx, tail, CONV_W - 1 - i) * cw_ref[i:i + 1, :]
    tail_ref[...] = gx[ROWS - 8:ROWS]
    r = _sigmoid(_dot(xc, wa_ref[...]) + ba_ref[...])
    gi = _sigmoid(_dot(xc, wx_ref[...]) + bx_ref[...])
    log_a = -RGLRU_C * r * _softplus(-lam_ref[...])
    valid = valid_ref[...] > 0.5
    a = jnp.where(valid, jnp.exp(log_a), 1.0)
    bb = jnp.where(valid, jnp.sqrt(-_expm1(2.0 * log_a)) * (gi * xc), 0.0)
    for dd in _seg_scan_steps(L):
        inside = posb_ref[...] >= dd
        a_s = jnp.where(inside, pltpu.roll(a, dd, 0), 1.0)
        b_s = jnp.where(inside, pltpu.roll(bb, dd, 0), 0.0)
        bb = a * b_s + bb
        a = a * a_s
    h = a * _dot_hi(segexp_ref[...], h_st[...]) + bb
    gg = z_ref[:, 256:512]
    y_ref[...] = (h * jax.nn.gelu(gg)).astype(y_ref.dtype)
    h_st[...] = _dot_hi(sellast_ref[...], h)

    @pl.when(c == C - 1)
    def _():
        h1_ref[0] = h_st[...]


def _rglru(z, h0, conv_w, conv_b, wa, ba, wx, bx, lam, cfg):
    S, C, L = cfg.S, cfg.C, cfg.L
    k = _consts(cfg)
    consts = [k["valid"], k["posb"], k["segexp"], k["sellast"]]
    hs = (ROWS, GROUP_W)
    return pl.pallas_call(
        functools.partial(_rglru_kernel, L=L, C=C), grid=(S, C),
        in_specs=[pl.BlockSpec((ROWS, 512), lambda s, c: (s * C + c, COL_G // 512)),
                  pl.BlockSpec((1,) + hs, lambda s, c: (s, 0, 0)),
                  _full((CONV_W, GROUP_W)), _full((1, GROUP_W)), _full((GROUP_W, GROUP_W)),
                  _full((1, GROUP_W)), _full((GROUP_W, GROUP_W)), _full((1, GROUP_W)),
                  _full((1, GROUP_W))] + [_full(a.shape) for a in consts],
        out_specs=[pl.BlockSpec((ROWS, GROUP_W), lambda s, c: (s * C + c, 0)),
                   pl.BlockSpec((1,) + hs, lambda s, c: (s, 0, 0))],
        out_shape=[jax.ShapeDtypeStruct((S * C * ROWS, GROUP_W), BF16),
                   jax.ShapeDtypeStruct((S,) + hs, F32)],
        scratch_shapes=[pltpu.VMEM((8, GROUP_W), F32), pltpu.VMEM(hs, F32)],
        compiler_params=_params(("parallel", "arbitrary")), name="rglru",
    )(z, h0, conv_w, conv_b, wa, ba, wx, bx, lam, *consts)


def _rwkv_kernel(z_ref, s0_ref, mu_ref, w0_ref, w2_ref, a0_ref, a2_ref, g2_ref, kk_ref, ka_ref,
                 rk_ref, lg_ref, lb_ref,
                 tri_ref, minc4_ref, mstr4_ref, seglast_ref, rowsel_ref, valid_ref, mv_ref, bd_ref,
                 avg_ref, ones_ref, segx_ref,
                 y_ref, s1_ref, tail_ref, s_st, *, G, L, C, stages):
    c = pl.program_id(1)

    @pl.when(c == 0)
    def _():
        tail_ref[...] = jnp.zeros_like(tail_ref)
        s_st[...] = s0_ref[0]

    z = z_ref[...]
    zprev = _shift_rows(z, tail_ref[...], 1)
    tail_ref[...] = z[ROWS - 8:ROWS]
    zs = z + (zprev - z) * mu_ref[...]
    valid = valid_ref[...]
    r = zs[:, 0:256]
    k = zs[:, 256:512]
    v = zs[:, 512:768]
    zwa = zs[:, 768:896]
    zg = zs[:, 896:1024]
    logw = -_softplus(-(w0_ref[...] + _dot(jnp.tanh(zwa), w2_ref[...]))) - 0.5
    lw = -jnp.exp(logw) * valid
    a = _sigmoid(a0_ref[...] + _dot(zwa, a2_ref[...]))
    g = _dot(_sigmoid(zg), g2_ref[...])
    kk = k * kk_ref[...]
    kk = kk / jnp.maximum(jnp.sqrt(_dot_hi(kk * kk, ones_ref[...])), 1e-12) * valid
    k2 = k * (1.0 + (a - 1.0) * ka_ref[...])
    bonus = _dot_hi(r * k2 * rk_ref[...], ones_ref[...]) * v
    k2 = k2 * valid
    cum = _dot_hi(tri_ref[...], lw)
    e_in = jnp.exp(cum)
    e_out = jnp.exp(-cum)
    at = -kk * jnp.exp(cum - lw)
    rt = r * e_in
    bh = kk * a * e_out
    kh = k2 * e_out
    at4 = _stack_heads(at, mv_ref)
    rt4 = _stack_heads(rt, mv_ref)
    strict = mstr4_ref[...] > 0.5
    incl = minc4_ref[...] > 0.5
    a_ab = jnp.where(strict, _dot_nt(at4, bh), 0.0)
    a_ak = jnp.where(strict, _dot_nt(at4, kh), 0.0)
    a_rb = jnp.where(incl, _dot_nt(rt4, bh), 0.0)
    a_rk = jnp.where(incl, _dot_nt(rt4, kh), 0.0)
    n3 = a_ab.reshape(N_GH, ROWS, ROWS)
    eye = (lax.broadcasted_iota(jnp.int32, (ROWS, ROWS), 0)
           == lax.broadcasted_iota(jnp.int32, (ROWS, ROWS), 1)).astype(F32)
    t3 = n3 + eye[None]
    pw = n3
    for _ in range(stages - 1):
        pw = jnp.einsum("hij,hjk->hik", pw.astype(BF16), pw.astype(BF16), preferred_element_type=F32)
        t3 = t3 + jnp.einsum("hij,hjk->hik", t3.astype(BF16), pw.astype(BF16), preferred_element_type=F32)
    t4 = t3.reshape(N_GH * ROWS, ROWS)
    st = s_st[...]
    y0 = _select(_dot_nt(at, st), segx_ref, G) + _unstack_heads(_dot(a_ak, v), mv_ref)
    p = _unstack_heads(_dot(t4, y0), mv_ref)
    o = (_select(_dot_nt(rt, st), segx_ref, G) + _unstack_heads(_dot(a_rb, p), mv_ref)
         + _unstack_heads(_dot(a_rk, v), mv_ref))
    cum_end = _dot_hi(seglast_ref[...], cum)
    e_end = jnp.exp(cum_end - cum)
    upd = _dot_tn(_expand(p, segx_ref, G), kk * a * e_end) + _dot_tn(_expand(v, segx_ref, G), k2 * e_end)
    s_st[...] = st * _dot_hi(rowsel_ref[...], jnp.exp(cum_end)) + upd * bd_ref[...]
    yn = _head_norm(o, avg_ref, RWKV_GN_EPS) * lg_ref[...] + lb_ref[...]
    y_ref[...] = ((yn + bonus) * g).astype(y_ref.dtype)

    @pl.when(c == C - 1)
    def _():
        s1_ref[0] = s_st[...]


def _rwkv(z, s0, P, cfg):
    S, C, G, L = cfg.S, cfg.C, cfg.G, cfg.L
    k = _consts(cfg)
    consts = [k[n] for n in ("tri", "minc4", "mstr4", "seglast", "rowsel", "valid", "mv", "bd", "avg",
                             "blockones", "segx")]
    stages = max(1, int(np.ceil(np.log2(L - cfg.lo))))
    st_shape = (G * GROUP_W, GROUP_W)
    params = [P["mu"], P["w0"], P["w2p"], P["a0"], P["a2p"], P["g2"], P["k_k"], P["k_a"], P["r_k"],
              P["lnx_g"], P["lnx_b"]]
    return pl.pallas_call(
        functools.partial(_rwkv_kernel, G=G, L=L, C=C, stages=stages), grid=(S, C),
        in_specs=[pl.BlockSpec((ROWS, 1024), lambda s, c: (s * C + c, COL_W // 1024)),
                  pl.BlockSpec((1,) + st_shape, lambda s, c: (s, 0, 0))]
                 + [_full(a.shape) for a in params] + [_full(a.shape) for a in consts],
        out_specs=[pl.BlockSpec((ROWS, GROUP_W), lambda s, c: (s * C + c, 0)),
                   pl.BlockSpec((1,) + st_shape, lambda s, c: (s, 0, 0))],
        out_shape=[jax.ShapeDtypeStruct((S * C * ROWS, GROUP_W), BF16),
                   jax.ShapeDtypeStruct((S,) + st_shape, F32)],
        scratch_shapes=[pltpu.VMEM((8, 1024), F32), pltpu.VMEM(st_shape, F32)],
        compiler_params=_params(("parallel", "arbitrary")), name="rwkv7",
    )(z, s0, *params, *consts)


def _to_bd(x, row_perm=None):
    B = x.shape[0]
    bd = jnp.einsum("bhij,hg->bhigj", x, jnp.eye(N_GH, dtype=x.dtype)).reshape(B, GROUP_W, GROUP_W)
    if row_perm is not None:
        bd = bd[:, row_perm, :]
    return bd


def _from_bd(bd, row_inv=None):
    B = bd.shape[0]
    if row_inv is not None:
        bd = bd[:, row_inv, :]
    return jnp.einsum("bhihj->bhij", bd.reshape(B, N_GH, HEAD_DIM, N_GH, HEAD_DIM))


def _pad_rows(x, n):
    S = n
    G = x.shape[0] // S
    x = x.reshape((S, G) + x.shape[1:])
    pad = [(0, 0), (0, ROWS - G)] + [(0, 0)] * (x.ndim - 2)
    return jnp.pad(x, pad)


def _pad_lanes(x, w):
    return jnp.pad(x, [(0, 0)] * (x.ndim - 1) + [(0, w - x.shape[-1])])


def _in_proj_perm():
    key = _ret_key_std()
    ar = np.arange
    zero = N_IN
    gate_i = np.full(128, zero)
    gate_f = np.full(128, zero)
    for h in range(N_GH):
        gate_i[h] = gate_i[N_GH + h] = O_MI + h
        gate_f[h] = gate_f[N_GH + h] = O_MF + h
    cols = np.concatenate([
        O_MQ + ar(1024),
        O_RQ + key, O_RK + key, O_RV + ar(256), O_RG + ar(256),
        O_WZ + ar(1024),
        O_GX + ar(512),
        gate_i, gate_f])
    assert cols.shape[0] == Z_W
    return cols


def _layer_weights(l, W):
    r = lambda a: a.reshape(1, -1)
    w_in = jnp.pad(W["w_in"][l], ((0, 0), (0, 1)))[:, _in_proj_perm()].astype(BF16)
    gate = lambda v: _pad_lanes(jnp.concatenate([v, v])[None, :], 128)
    bdiag = lambda w: jnp.einsum("hij,hg->higj", w, jnp.eye(N_GH, dtype=w.dtype)).reshape(GROUP_W, GROUP_W)
    z64 = jnp.zeros((64, GROUP_W), F32)
    wo = W["w_out"][l].astype(BF16)
    return dict(
        w_in=w_in, bi=gate(W["mlstm_b_i"][l]), bf=gate(W["mlstm_b_f"][l]),
        mlstm_g=r(W["mlstm_g"][l]), ret_g=r(W["ret_g"][l]),
        conv_w=W["conv_w"][l], conv_b=r(W["conv_b"][l]),
        wa=bdiag(W["rg_w_a"][l]).astype(BF16), ba=r(W["rg_b_a"][l]),
        wx=bdiag(W["rg_w_x"][l]).astype(BF16), bx=r(W["rg_b_x"][l]), lam=r(W["rg_lam"][l]),
        rw=dict(mu=r(W["rwkv_mu"][l]), w0=r(W["rwkv_w0"][l]),
                w2p=jnp.concatenate([W["rwkv_w2"][l], z64], 0).astype(BF16),
                a0=r(W["rwkv_a0"][l]),
                a2p=jnp.concatenate([z64, W["rwkv_a2"][l]], 0).astype(BF16),
                g2=W["rwkv_g2"][l].astype(BF16), k_k=r(W["rwkv_k_k"][l]), k_a=r(W["rwkv_k_a"][l]),
                r_k=r(W["rwkv_r_k"][l]), lnx_g=r(W["rwkv_lnx_g"][l]), lnx_b=r(W["rwkv_lnx_b"][l])),
        w_out=[wo[i * GROUP_W:(i + 1) * GROUP_W] for i in range(4)],
        ln1=(r(W["ln1_g"][l]), r(W["ln1_b"][l])), ln2=(r(W["ln2_g"][l]), r(W["ln2_b"][l])),
        ln3=(r(W["ln3_g"][l]), r(W["ln3_b"][l])),
        wq=W["xa_wq"][l].astype(BF16), wo=W["xa_wo"][l].astype(BF16),
        wkv=jnp.concatenate([W["xa_wk"][l], W["xa_wv"][l]], 1).astype(BF16),
        w_up=W["w_up"][l].astype(BF16), w_down=W["w_down"][l].astype(BF16))


def _rope_tables(pos):
    half = HEAD_DIM // 2
    inv = ROPE_BASE ** (-jnp.arange(half, dtype=F32) / half)
    ang = pos.astype(F32)[:, None] * inv[None, :]
    return jnp.tile(jnp.cos(ang), (1, N_GH)), jnp.tile(jnp.sin(ang), (1, N_GH))


def _layer(x, z_fix, state, mem_k, mem_v, rope, LW, cfg, tm, attn_G, attn_tq):
    c0, n0, m0, r0, h0, s0 = state
    z = _matmul(x, LW["w_in"], F32, tm, "in_proj")
    z = z_fix(z)
    y_m, c1, n1, m1 = _mlstm(z, LW["bi"], LW["bf"], LW["mlstm_g"], c0, n0, m0, cfg)
    y_r, r1 = _retention(z, rope[0], rope[1], r0, LW["ret_g"], cfg)
    y_g, h1 = _rglru(z, h0, LW["conv_w"], LW["conv_b"], LW["wa"], LW["ba"], LW["wx"], LW["bx"],
                     LW["lam"], cfg)
    y_w, s1 = _rwkv(z, s0, LW["rw"], cfg)
    x = _proj_res_ln([y_m, y_r, y_g, y_w], LW["w_out"], x, *LW["ln1"], tm, "out_proj_ln")
    B = mem_k.shape[0]
    q = _matmul(x, LW["wq"], BF16, tm, "q_proj").reshape(B, -1, D_MODEL)
    o = _cross_attn(q, mem_k, mem_v, attn_G, attn_tq).reshape(-1, D_MODEL)
    x = _proj_res_ln([o], [LW["wo"]], x, *LW["ln2"], tm, "attn_out_ln")
    x = _mlp(x, LW["w_up"], LW["w_down"], *LW["ln3"], tm)
    return x, (c1, n1, m1, r1, h1, s1), z


def kernel(x_prompt, x_sample, state_mlstm_C, state_mlstm_n, state_mlstm_m, state_ret, state_rglru_h, state_rglru_conv, state_rwkv_S, state_rwkv_shift, cache_mem_k, cache_mem_v, mem_prompt, ln_in_g, ln_in_b, w_in, mlstm_b_i, mlstm_b_f, mlstm_g, ret_g, conv_w, conv_b, rg_w_a, rg_b_a, rg_w_x, rg_b_x, rg_lam, rwkv_mu, rwkv_w0, rwkv_w2, rwkv_a0, rwkv_a2, rwkv_g2, rwkv_k_k, rwkv_k_a, rwkv_r_k, rwkv_lnx_g, rwkv_lnx_b, w_out, ln1_g, ln1_b, xa_wq, xa_wk, xa_wv, xa_wo, ln2_g, ln2_b, w_up, w_down, ln3_g, ln3_b):
    W = dict(w_in=w_in, mlstm_b_i=mlstm_b_i, mlstm_b_f=mlstm_b_f, mlstm_g=mlstm_g, ret_g=ret_g,
             conv_w=conv_w, conv_b=conv_b, rg_w_a=rg_w_a, rg_b_a=rg_b_a, rg_w_x=rg_w_x, rg_b_x=rg_b_x,
             rg_lam=rg_lam, rwkv_mu=rwkv_mu, rwkv_w0=rwkv_w0, rwkv_w2=rwkv_w2, rwkv_a0=rwkv_a0,
             rwkv_a2=rwkv_a2, rwkv_g2=rwkv_g2, rwkv_k_k=rwkv_k_k, rwkv_k_a=rwkv_k_a, rwkv_r_k=rwkv_r_k,
             rwkv_lnx_g=rwkv_lnx_g, rwkv_lnx_b=rwkv_lnx_b, w_out=w_out, ln1_g=ln1_g, ln1_b=ln1_b,
             xa_wq=xa_wq, xa_wk=xa_wk, xa_wv=xa_wv, xa_wo=xa_wo, ln2_g=ln2_g, ln2_b=ln2_b,
             w_up=w_up, w_down=w_down, ln3_g=ln3_g, ln3_b=ln3_b)
    Bp, Tp, _ = x_prompt.shape
    Bs, Ts, _ = x_sample.shape
    assert Ts == DEC_SEQ and Tp % ROWS == 0 and Bs % (ROWS // SAMPLE_L) == 0
    Gs = ROWS // SAMPLE_L
    cfg_p = _Cfg(S=Bp, C=Tp // ROWS, G=1, L=ROWS, lo=0)
    cfg_s = _Cfg(S=Bs // Gs, C=1, G=Gs, L=SAMPLE_L, lo=SAMPLE_LO)
    key_std = _ret_key_std()
    key_inv = np.argsort(key_std)
    tm_p = 256
    tm_s = min(256, Bs * SAMPLE_L)

    g_in, b_in = ln_in_g.reshape(1, -1), ln_in_b.reshape(1, -1)
    xp = _layer_norm(x_prompt.reshape(Bp * Tp, D_MODEL), g_in, b_in, tm_p)
    xs_pad = jnp.pad(x_sample, ((0, 0), (SAMPLE_LO, 0), (0, 0))).reshape(Bs * SAMPLE_L, D_MODEL)
    xs = _layer_norm(xs_pad, g_in, b_in, tm_s)

    rope_p = _rope_tables(jnp.arange(Tp, dtype=jnp.int32))
    pos_s = PAST_LEN + jnp.tile(jnp.maximum(jnp.arange(SAMPLE_L, dtype=jnp.int32) - SAMPLE_LO, 0), Gs)
    rope_s = _rope_tables(pos_s)
    mem2d = mem_prompt.reshape(Bp * N_MEM, D_MODEL)

    zero_bd = jnp.zeros((Bp, GROUP_W, GROUP_W), F32)
    zero_state = (zero_bd, jnp.zeros((Bp, ROWS, GROUP_W), F32), jnp.zeros((Bp, ROWS, 128), F32),
                  zero_bd, jnp.zeros((Bp, ROWS, GROUP_W), F32), zero_bd)
    blk = lambda bd: bd.reshape(cfg_s.S, Gs * GROUP_W, GROUP_W)
    unblk = lambda x: x.reshape(Bs, GROUP_W, GROUP_W)

    outs_p, outs_s, mk_p, mv_p = [], [], [], []
    for l in range(DEPTH):
        LW = _layer_weights(l, W)
        kv = _matmul(mem2d, LW["wkv"], F32, 256, "mem_kv_proj")
        kp = kv[:, :D_MODEL].reshape(Bp, N_MEM, D_MODEL)
        vp = kv[:, D_MODEL:].reshape(Bp, N_MEM, D_MODEL)
        mk_p.append(kp.reshape(Bp, N_MEM, XA_HEADS, XA_HEAD_DIM))
        mv_p.append(vp.reshape(Bp, N_MEM, XA_HEADS, XA_HEAD_DIM))

        xp, st, z = _layer(xp, lambda z: z, zero_state, kp, vp, rope_p, LW, cfg_p, tm_p, 1, min(512, Tp))
        c1, n1, m1, r1, h1, s1 = st
        z3 = z.reshape(Bp, Tp, Z_W)
        outs_p.append((
            _from_bd(c1), n1[:, 0, :].reshape(Bp, N_GH, HEAD_DIM), m1[:, 0, :N_GH],
            _from_bd(r1, key_inv), h1[:, 0, :],
            z3[:, Tp - (CONV_W - 1):, COL_G:COL_G + GROUP_W], _from_bd(s1),
            z3[:, Tp - 1, COL_W:COL_W + 1024]))

        conv0, shift0 = state_rglru_conv[l], state_rwkv_shift[l]

        def z_fix(z, conv0=conv0, shift0=shift0):
            z3 = z.reshape(Bs, SAMPLE_L, Z_W)
            z3 = z3.at[:, SAMPLE_LO - (CONV_W - 1):SAMPLE_LO, COL_G:COL_G + GROUP_W].set(conv0)
            z3 = z3.at[:, SAMPLE_LO - 1, COL_W:COL_W + 1024].set(shift0)
            return z3.reshape(Bs * SAMPLE_L, Z_W)

        st_in = (blk(_to_bd(state_mlstm_C[l])),
                 _pad_rows(state_mlstm_n[l].reshape(Bs, GROUP_W), cfg_s.S),
                 _pad_rows(_pad_lanes(jnp.concatenate([state_mlstm_m[l]] * 2, -1), 128), cfg_s.S),
                 blk(_to_bd(state_ret[l], key_std)),
                 _pad_rows(state_rglru_h[l], cfg_s.S),
                 blk(_to_bd(state_rwkv_S[l])))
        ck = cache_mem_k[l].reshape(Bs, N_MEM, D_MODEL)
        cv = cache_mem_v[l].reshape(Bs, N_MEM, D_MODEL)
        xs, st, z = _layer(xs, z_fix, st_in, ck, cv, rope_s, LW, cfg_s, tm_s, 4, SAMPLE_L)
        c1, n1, m1, r1, h1, s1 = st
        z3 = z.reshape(Bs, SAMPLE_L, Z_W)
        outs_s.append((
            _from_bd(unblk(c1)), n1[:, :Gs, :].reshape(Bs, N_GH, HEAD_DIM),
            m1[:, :Gs, :N_GH].reshape(Bs, N_GH),
            _from_bd(unblk(r1), key_inv), h1[:, :Gs, :].reshape(Bs, GROUP_W),
            z3[:, SAMPLE_L - (CONV_W - 1):, COL_G:COL_G + GROUP_W], _from_bd(unblk(s1)),
            z3[:, SAMPLE_L - 1, COL_W:COL_W + 1024]))

    stack = lambda outs: [jnp.stack(s) for s in zip(*outs)]
    p_C, p_n, p_m, p_R, p_h, p_conv, p_S, p_shift = stack(outs_p)
    s_C, s_n, s_m, s_R, s_h, s_conv, s_S, s_shift = stack(outs_s)
    y_p = xp.reshape(Bp, Tp, D_MODEL)
    y_s = xs.reshape(Bs, SAMPLE_L, D_MODEL)[:, SAMPLE_LO:, :]
    return (y_p, y_s, p_C, p_n, p_m, p_R, p_h, p_conv, p_S, p_shift, jnp.stack(mk_p), jnp.stack(mv_p),
            s_C, s_n, s_m, s_R, s_h, s_conv, s_S, s_shift)
```
